```python
import math
import jax, jax.numpy as jnp
from jax import lax
import numpy as np

D_MODEL = 1024
BATCH = 4
SEQ = 8192
DEPTH = 4

CHUNK = 128
SSD_INNER = D_MODEL
SSD_HEAD_DIM = 64
SSD_HEADS = SSD_INNER // SSD_HEAD_DIM
SSD_GROUPS = 2
SSD_STATE = 128
SSD_CONV = 4
SSD_CONV_CH = SSD_INNER + 2 * SSD_GROUPS * SSD_STATE
SGU_INNER = D_MODEL
SGU_GROUPS = 8
SGU_GROUP_DIM = SGU_INNER // SGU_GROUPS
EVEN_IN = SSD_INNER + SSD_CONV_CH + SSD_HEADS + 2 * SGU_INNER
EVEN_SPLITS = (SSD_INNER, SSD_INNER + SSD_CONV_CH, SSD_INNER + SSD_CONV_CH + SSD_HEADS,
               SSD_INNER + SSD_CONV_CH + SSD_HEADS + SGU_INNER)
EVEN_MIX = SSD_INNER + SGU_INNER
CONV_INNER = D_MODEL
CONV_WIDTH = 3
N_EXPERTS = 32
N_EXPERT_GROUPS = 4
EXPERTS_PER_GROUP = N_EXPERTS // N_EXPERT_GROUPS
TOP_K = 2
GROUP_SCORE_TOPK = 2
D_EXPERT = D_MODEL
MOE_BLOCK = 128
ALPHA = (2 * DEPTH) ** 0.25
BETA = (8 * DEPTH) ** -0.25
N_EVEN = (DEPTH + 1) // 2
N_ODD = DEPTH // 2
LN_EPS = 1e-5
RMS_EPS = 1e-5

kernel_name = "hybrid_ssd_sgu_shortconv_grouped_moe_deepnorm"


def layer_norm(x, g, b):
    xf = x.astype(jnp.float32)
    mu = jnp.mean(xf, axis=-1, keepdims=True)
    var = jnp.mean(jnp.square(xf - mu), axis=-1, keepdims=True)
    return ((xf - mu) * lax.rsqrt(var + LN_EPS) * g + b).astype(x.dtype)


def causal_depthwise_conv(x, w):
    k, c = w.shape
    return lax.conv_general_dilated(
        x, w[:, None, :].astype(x.dtype), window_strides=(1,), padding=[(k - 1, 0)],
        dimension_numbers=("NWC", "WIO", "NWC"), feature_group_count=c)


def ssd_chunked(xs, da, b_ssm, c_ssm):
    bsz, seq, h, p = xs.shape
    g, n = b_ssm.shape[-2:]
    r = h // g
    c = seq // CHUNK
    X = xs.reshape(bsz, c, CHUNK, g, r, p)
    A = da.reshape(bsz, c, CHUNK, g, r)
    Bc = b_ssm.reshape(bsz, c, CHUNK, g, n)
    Cc = c_ssm.reshape(bsz, c, CHUNK, g, n)
    a_cum = jnp.cumsum(A, axis=2)
    causal = jnp.tril(jnp.ones((CHUNK, CHUNK), dtype=bool))
    seg = a_cum[:, :, :, None] - a_cum[:, :, None, :]
    decay = jnp.exp(jnp.where(causal[None, None, :, :, None, None], seg, -jnp.inf))
    cb = jnp.einsum("bctgn,bcsgn->bctsg", Cc, Bc)
    y_diag = jnp.einsum("bctsgr,bcsgrp->bctgrp", cb[..., None] * decay, X)
    decay_to_end = jnp.exp(a_cum[:, :, -1:] - a_cum)
    states = jnp.einsum("bclgn,bclgrp->bcgrpn", Bc, decay_to_end[..., None] * X)
    chunk_decay = jnp.exp(a_cum[:, :, -1])

    def step(carry, inp):
        st, dec = inp
        return carry * dec[..., None, None] + st, carry

    init = jnp.zeros((bsz, g, r, p, n), xs.dtype)
    _, prev = lax.scan(step, init, (jnp.moveaxis(states, 1, 0), jnp.moveaxis(chunk_decay, 1, 0)))
    prev = jnp.moveaxis(prev, 0, 1)
    y_off = jnp.einsum("bclgn,bcgrpn->bclgrp", Cc, prev) * jnp.exp(a_cum)[..., None]
    return (y_diag + y_off).reshape(bsz, seq, h, p)


def even_mixer(x, w_in, conv_w, conv_b, dt_bias, a_log, d_skip, ssd_norm_w,
               sgu_ln_g, sgu_ln_b, sgu_w, sgu_b, w_out):
    f32 = jnp.float32
    bsz, seq, _ = x.shape
    n_chunks = seq // CHUNK
    proj = x @ w_in
    z, xbc, dt, u, v = jnp.split(proj, EVEN_SPLITS, axis=-1)
    xbc = jax.nn.silu(causal_depthwise_conv(xbc, conv_w) + conv_b)
    xs, b_ssm, c_ssm = jnp.split(xbc, (SSD_INNER, SSD_INNER + SSD_GROUPS * SSD_STATE), axis=-1)
    xs = xs.reshape(bsz, seq, SSD_HEADS, SSD_HEAD_DIM).astype(f32)
    b_ssm = b_ssm.reshape(bsz, seq, SSD_GROUPS, SSD_STATE).astype(f32)
    c_ssm = c_ssm.reshape(bsz, seq, SSD_GROUPS, SSD_STATE).astype(f32)
    dt = jax.nn.softplus(dt.astype(f32) + dt_bias.astype(f32))
    a = -jnp.exp(a_log.astype(f32))
    y = ssd_chunked(xs * dt[..., None], dt * a, b_ssm, c_ssm)
    y = y + d_skip.astype(f32)[:, None] * xs
    grp_shape = (bsz, seq, SSD_GROUPS, SSD_INNER // SSD_GROUPS)
    y = y.reshape(grp_shape) * jax.nn.silu(z.astype(f32)).reshape(grp_shape)
    y = y * lax.rsqrt(jnp.mean(jnp.square(y), axis=-1, keepdims=True) + RMS_EPS)
    y_ssd = (y.reshape(bsz, seq, SSD_INNER) * ssd_norm_w).astype(x.dtype)
    u = jax.nn.gelu(u)
    v = layer_norm(jax.nn.gelu(v).reshape(bsz, seq, SGU_GROUPS, SGU_GROUP_DIM), sgu_ln_g, sgu_ln_b)
    v = v.reshape(bsz, n_chunks, CHUNK, SGU_GROUPS, SGU_GROUP_DIM)
    w_s = jnp.tril(sgu_w).astype(v.dtype)
    s = jnp.einsum("gts,bcsgd->bctgd", w_s, v) + sgu_b.T[:, :, None]
    y_sgu = u * s.reshape(bsz, seq, SGU_INNER).astype(u.dtype)
    return jnp.concatenate([y_ssd, y_sgu], axis=-1) @ w_out


def conv_mixer(x, w_in, conv_w, w_out):
    gate_b, gate_c, h = jnp.split(x @ w_in, 3, axis=-1)
    y = gate_b * causal_depthwise_conv(gate_c * h, conv_w)
    return y @ w_out


def moe(x, router_w, router_b, w_gate, w_up, w_down):
    bsz, seq, d = x.shape
    h = x.reshape(-1, d)
    t = h.shape[0]
    tk = t * TOP_K
    scores = jax.nn.sigmoid((h @ router_w).astype(jnp.float32))
    biased = (scores + router_b.astype(jnp.float32)).reshape(t, N_EXPERT_GROUPS, EXPERTS_PER_GROUP)
    group_score = lax.top_k(biased, GROUP_SCORE_TOPK)[0].sum(-1)
    sel_group = jnp.argmax(group_score, axis=-1)
    in_group = jnp.take_along_axis(biased, sel_group[:, None, None], axis=1)[:, 0]
    _, local = lax.top_k(in_group, TOP_K)
    expert = sel_group[:, None] * EXPERTS_PER_GROUP + local
    gate = jnp.take_along_axis(scores, expert, axis=1)
    gate = gate / jnp.sum(gate, axis=-1, keepdims=True)
    flat_e = expert.reshape(-1)
    order = jnp.argsort(flat_e)
    se = flat_e[order]
    tok = order // TOP_K
    counts = jnp.bincount(flat_e, length=N_EXPERTS)
    padded = (counts + MOE_BLOCK - 1) // MOE_BLOCK * MOE_BLOCK
    pad_end = jnp.cumsum(padded)
    pad_start = pad_end - padded
    start = jnp.cumsum(counts) - counts
    dest = pad_start[se] + jnp.arange(tk) - start[se]
    n_blocks = -(-tk // MOE_BLOCK) + N_EXPERTS
    rows = n_blocks * MOE_BLOCK
    buf = jnp.zeros((rows, d), x.dtype).at[dest].set(h[tok])
    block_e = jnp.minimum(jnp.searchsorted(pad_end, jnp.arange(n_blocks) * MOE_BLOCK, side="right"),
                          N_EXPERTS - 1)

    def run_block(args):
        xb, e = args
        return (jax.nn.silu(xb @ w_gate[e]) * (xb @ w_up[e])) @ w_down[e]

    out = lax.map(run_block, (buf.reshape(n_blocks, MOE_BLOCK, d), block_e))
    y_sorted = out.reshape(rows, d)[dest] * gate.reshape(-1)[order][:, None]
    y = jnp.zeros((t, d), x.dtype).at[tok].add(y_sorted.astype(x.dtype))
    return y.reshape(bsz, seq, d)


def setup_inputs(seed: int = 0) -> dict:
    key = jax.random.key(seed)
    ks = jax.random.split(key, 26)
    nrm = jax.random.normal
    f32 = jnp.float32
    x = nrm(ks[0], (BATCH, SEQ, D_MODEL), f32)
    even_w_in = nrm(ks[1], (N_EVEN, D_MODEL, EVEN_IN), f32) * D_MODEL ** -0.5
    even_conv_w = nrm(ks[2], (N_EVEN, SSD_CONV, SSD_CONV_CH), f32) * SSD_CONV ** -0.5
    even_conv_b = 0.02 * nrm(ks[3], (N_EVEN, SSD_CONV_CH), f32)
    dt0 = jnp.exp(jax.random.uniform(ks[4], (N_EVEN, SSD_HEADS), f32, math.log(1e-3), math.log(1e-1)))
    dt_bias = dt0 + jnp.log(-jnp.expm1(-dt0))
    a_log = jnp.log(jax.random.uniform(ks[5], (N_EVEN, SSD_HEADS), f32, 1.0, 16.0))
    d_skip = 1.0 + 0.1 * nrm(ks[6], (N_EVEN, SSD_HEADS), f32)
    ssd_norm_w = 1.0 + 0.02 * nrm(ks[7], (N_EVEN, SSD_INNER), f32)
    sgu_ln_g = 1.0 + 0.02 * nrm(ks[8], (N_EVEN, SGU_GROUPS, SGU_GROUP_DIM), f32)
    sgu_ln_b = 0.02 * nrm(ks[9], (N_EVEN, SGU_GROUPS, SGU_GROUP_DIM), f32)
    sgu_w = nrm(ks[10], (N_EVEN, SGU_GROUPS, CHUNK, CHUNK), f32) * CHUNK ** -0.5
    sgu_b = 1.0 + 0.1 * nrm(ks[11], (N_EVEN, SGU_GROUPS, CHUNK), f32)
    even_w_out = nrm(ks[12], (N_EVEN, EVEN_MIX, D_MODEL), f32) * EVEN_MIX ** -0.5 * BETA
    odd_w_in = nrm(ks[13], (N_ODD, D_MODEL, 3 * CONV_INNER), f32) * D_MODEL ** -0.5
    odd_conv_w = nrm(ks[14], (N_ODD, CONV_WIDTH, CONV_INNER), f32) * CONV_WIDTH ** -0.5
    odd_w_out = nrm(ks[15], (N_ODD, CONV_INNER, D_MODEL), f32) * CONV_INNER ** -0.5 * BETA
    mix_ln_g = 1.0 + 0.02 * nrm(ks[16], (DEPTH, D_MODEL), f32)
    mix_ln_b = 0.02 * nrm(ks[17], (DEPTH, D_MODEL), f32)
    ffn_ln_g = 1.0 + 0.02 * nrm(ks[18], (DEPTH, D_MODEL), f32)
    ffn_ln_b = 0.02 * nrm(ks[19], (DEPTH, D_MODEL), f32)
    router_w = nrm(ks[20], (D_MODEL, N_EXPERTS), f32) * D_MODEL ** -0.5
    router_b = 0.01 * nrm(ks[21], (N_EXPERTS,), f32)
    moe_w_gate = nrm(ks[22], (DEPTH, N_EXPERTS, D_MODEL, D_EXPERT), f32) * D_MODEL ** -0.5
    moe_w_up = nrm(ks[23], (DEPTH, N_EXPERTS, D_MODEL, D_EXPERT), f32) * D_MODEL ** -0.5
    moe_w_down = nrm(ks[24], (DEPTH, N_EXPERTS, D_EXPERT, D_MODEL), f32) * D_EXPERT ** -0.5 * BETA
    return {"x": x, "even_w_in": even_w_in, "even_conv_w": even_conv_w, "even_conv_b": even_conv_b,
            "dt_bias": dt_bias, "a_log": a_log, "d_skip": d_skip, "ssd_norm_w": ssd_norm_w,
            "sgu_ln_g": sgu_ln_g, "sgu_ln_b": sgu_ln_b, "sgu_w": sgu_w, "sgu_b": sgu_b,
            "even_w_out": even_w_out, "odd_w_in": odd_w_in, "odd_conv_w": odd_conv_w,
            "odd_w_out": odd_w_out, "mix_ln_g": mix_ln_g, "mix_ln_b": mix_ln_b,
            "ffn_ln_g": ffn_ln_g, "ffn_ln_b": ffn_ln_b, "router_w": router_w, "router_b": router_b,
            "moe_w_gate": moe_w_gate, "moe_w_up": moe_w_up, "moe_w_down": moe_w_down}


def reference(x, even_w_in, even_conv_w, even_conv_b, dt_bias, a_log, d_skip, ssd_norm_w,
              sgu_ln_g, sgu_ln_b, sgu_w, sgu_b, even_w_out, odd_w_in, odd_conv_w, odd_w_out,
              mix_ln_g, mix_ln_b, ffn_ln_g, ffn_ln_b, router_w, router_b,
              moe_w_gate, moe_w_up, moe_w_down):
    for i in range(DEPTH):
        j = i // 2
        if i % 2 == 0:
            m = even_mixer(x, even_w_in[j], even_conv_w[j], even_conv_b[j], dt_bias[j], a_log[j],
                           d_skip[j], ssd_norm_w[j], sgu_ln_g[j], sgu_ln_b[j], sgu_w[j], sgu_b[j],
                           even_w_out[j])
        else:
            m = conv_mixer(x, odd_w_in[j], odd_conv_w[j], odd_w_out[j])
        x = layer_norm(ALPHA * x + m, mix_ln_g[i], mix_ln_b[i])
        f = moe(x, router_w, router_b, moe_w_gate[i], moe_w_up[i], moe_w_down[i])
        x = layer_norm(ALPHA * x + f, ffn_ln_g[i], ffn_ln_b[i])
    return x
```

```python
import functools

import jax
import jax.numpy as jnp
from jax import lax
from jax.experimental import pallas as pl
from jax.experimental.pallas import tpu as pltpu

F32 = jnp.float32
BF16 = jnp.bfloat16
HIGHEST = lax.Precision.HIGHEST

V7X_LANES = 128
V7X_SUBLANES = 8
V7X_VMEM_BYTES = 64 * 1024 * 1024
VMEM_LIMIT_CAP = V7X_VMEM_BYTES - 8 * 1024 * 1024

CHUNK = 128
SSD_HEAD_DIM = 64
SSD_HEADS = 16
SSD_GROUPS = 2
SSD_STATE = 128
SSD_CONV = 4
SGU_GROUPS = 8
CONV_WIDTH = 3
N_EXPERTS = 32
N_EXPERT_GROUPS = 4
EXPERTS_PER_GROUP = 8
TOP_K = 2
LN_EPS = 1e-5
RMS_EPS = 1e-5
NEG_BIG = -1e30

ROW_TILE = 512
MOE_ROWS = 256


def _dot(a, b):
    return jnp.dot(a, b, preferred_element_type=F32)


def _silu(x):
    return x * jax.nn.sigmoid(x)


def _layer_norm_rows(y, g, b):
    mu = jnp.mean(y, axis=-1, keepdims=True)
    d = y - mu
    var = jnp.mean(d * d, axis=-1, keepdims=True)
    return d * lax.rsqrt(var + LN_EPS) * g + b


def _const_spec(shape):
    zeros = (0,) * len(shape)
    return pl.BlockSpec(shape, lambda *_: zeros, pipeline_mode=pl.Buffered(1))


def _params(n_axes, vmem_bytes):
    return pltpu.CompilerParams(
        dimension_semantics=("arbitrary",) * n_axes,
        vmem_limit_bytes=min(int(vmem_bytes), VMEM_LIMIT_CAP))


def _route(xn, rwt_ref, rb_ref, e_ref, g_ref):
    logits = lax.dot_general(rwt_ref[...], xn, (((1,), (1,)), ((), ())),
                             precision=HIGHEST, preferred_element_type=F32)
    scores = jax.nn.sigmoid(logits)
    biased = scores + rb_ref[...]
    n = xn.shape[0]
    sub = lax.broadcasted_iota(jnp.int32, (EXPERTS_PER_GROUP, n), 0)
    best = sel = l1 = l2 = s1 = s2 = None
    for g in range(N_EXPERT_GROUPS):
        bg = biased[g * EXPERTS_PER_GROUP:(g + 1) * EXPERTS_PER_GROUP]
        sg = scores[g * EXPERTS_PER_GROUP:(g + 1) * EXPERTS_PER_GROUP]
        m1 = jnp.max(bg, axis=0, keepdims=True)
        i1 = jnp.min(jnp.where(bg == m1, sub, EXPERTS_PER_GROUP), axis=0, keepdims=True)
        rest = jnp.where(sub == i1, NEG_BIG, bg)
        m2 = jnp.max(rest, axis=0, keepdims=True)
        i2 = jnp.min(jnp.where(rest == m2, sub, EXPERTS_PER_GROUP), axis=0, keepdims=True)
        gs = m1 + m2
        v1 = jnp.sum(jnp.where(sub == i1, sg, 0.0), axis=0, keepdims=True)
        v2 = jnp.sum(jnp.where(sub == i2, sg, 0.0), axis=0, keepdims=True)
        if g == 0:
            best, sel, l1, l2, s1, s2 = gs, jnp.zeros_like(i1), i1, i2, v1, v2
        else:
            better = gs > best
            best = jnp.where(better, gs, best)
            sel = jnp.where(better, g, sel)
            l1 = jnp.where(better, i1, l1)
            l2 = jnp.where(better, i2, l2)
            s1 = jnp.where(better, v1, s1)
            s2 = jnp.where(better, v2, s2)
    tot = s1 + s2
    e_ref[0:1, :] = sel * EXPERTS_PER_GROUP + l1
    e_ref[1:2, :] = sel * EXPERTS_PER_GROUP + l2
    g_ref[0:1, :] = s1 / tot
    g_ref[1:2, :] = s2 / tot


def _even_in_kernel(x_ref, w_ref, lng_ref, lnb_ref, zs_ref, xbc_ref, dt_ref, ug_ref, vn_ref):
    d = x_ref.shape[1]
    xb = x_ref[...].astype(BF16)
    o_xbc = d
    o_dt = o_xbc + xbc_ref.shape[1]
    o_u = o_dt + d
    o_v = o_u + d
    zs_ref[...] = _silu(_dot(xb, w_ref[:, 0:d])).astype(BF16)
    xbc_ref[...] = _dot(xb, w_ref[:, o_xbc:o_dt]).astype(BF16)
    dt_ref[...] = _dot(xb, w_ref[:, o_dt:o_u])
    ug_ref[...] = jax.nn.gelu(_dot(xb, w_ref[:, o_u:o_v])).astype(BF16)
    gv = jax.nn.gelu(_dot(xb, w_ref[:, o_v:o_v + d]))
    gd = d // SGU_GROUPS
    for g in range(SGU_GROUPS):
        sl = slice(g * gd, (g + 1) * gd)
        vn_ref[:, sl] = _layer_norm_rows(gv[:, sl], lng_ref[:, sl], lnb_ref[:, sl]).astype(BF16)


def _even_in(x, w_all, sgu_g, sgu_b, n_xbc):
    t, d = x.shape
    tm = ROW_TILE
    n_all = w_all.shape[1]
    row = lambda i: (i, 0)
    vmem = (2 * tm * d * 4 + n_all * d * 2 + 2 * tm * (3 * d + n_xbc) * 2 + 2 * tm * d * 4
            + 6 * tm * d * 4)
    return pl.pallas_call(
        _even_in_kernel,
        grid=(t // tm,),
        in_specs=[pl.BlockSpec((tm, d), row), _const_spec((d, n_all)),
                  _const_spec((1, d)), _const_spec((1, d))],
        out_specs=[pl.BlockSpec((tm, d), row), pl.BlockSpec((tm, n_xbc), row),
                   pl.BlockSpec((tm, d), row), pl.BlockSpec((tm, d), row),
                   pl.BlockSpec((tm, d), row)],
        out_shape=[jax.ShapeDtypeStruct((t, d), BF16), jax.ShapeDtypeStruct((t, n_xbc), BF16),
                   jax.ShapeDtypeStruct((t, d), F32), jax.ShapeDtypeStruct((t, d), BF16),
                   jax.ShapeDtypeStruct((t, d), BF16)],
        compiler_params=_params(1, vmem),
        name="even_in_proj",
    )(x, w_all, sgu_g, sgu_b)


def _ssd_kernel(xbc_ref, dt_ref, zs_ref, cw_ref, cb_ref, dtb_ref, alog_ref, dsk_ref, nw_ref,
                y_ref, s_ref, tail_ref):
    d = y_ref.shape[1]
    gw = d // SSD_GROUPS
    n = SSD_STATE

    @pl.when(pl.program_id(1) == 0)
    def _():
        s_ref[...] = jnp.zeros_like(s_ref)
        tail_ref[...] = jnp.zeros_like(tail_ref)

    xin = xbc_ref[...].astype(F32)
    ext = jnp.concatenate([tail_ref[...], xin], axis=0)
    tail_ref[...] = xin[CHUNK - V7X_SUBLANES:CHUNK]
    acc = cb_ref[...] + cw_ref[SSD_CONV - 1:SSD_CONV, :] * xin
    for k in range(SSD_CONV - 1):
        off = V7X_SUBLANES - (SSD_CONV - 1) + k
        acc = acc + cw_ref[k:k + 1, :] * ext[off:off + CHUNK]
    xbc = _silu(acc)
    xs = xbc[:, 0:d]
    bm = xbc[:, d:d + SSD_GROUPS * n]
    cm = xbc[:, d + SSD_GROUPS * n:d + 2 * SSD_GROUPS * n]

    dtr = dt_ref[...] + dtb_ref[...]
    dt = jnp.maximum(dtr, 0.0) + jnp.log(1.0 + jnp.exp(-jnp.abs(dtr)))
    da = dt * (-jnp.exp(alog_ref[...]))
    row_i = lax.broadcasted_iota(jnp.int32, (CHUNK, CHUNK), 0)
    col_i = lax.broadcasted_iota(jnp.int32, (CHUNK, CHUNK), 1)
    tril = row_i >= col_i
    a_cum = jnp.dot(tril.astype(F32), da, precision=HIGHEST, preferred_element_type=F32)
    a_last = a_cum[CHUNK - 1:CHUNK, :]
    ea = jnp.exp(a_cum)
    dte = jnp.exp(a_last - a_cum)
    xdt = xs * dt
    xd_b = (xdt * dte).astype(BF16)
    lane_lo = col_i < SSD_HEAD_DIM

    y_parts = []
    for g in range(SSD_GROUPS):
        gsl = slice(g * gw, (g + 1) * gw)
        b_g = bm[:, g * n:(g + 1) * n]
        c_gb = cm[:, g * n:(g + 1) * n].astype(BF16)
        cbm = lax.dot_general(c_gb, b_g.astype(BF16), (((1,), (1,)), ((), ())),
                              preferred_element_type=F32)
        s_prev = s_ref[:, gsl]
        y_off = _dot(c_gb, s_prev.astype(BF16)) * ea[:, gsl]
        s_new = _dot(b_g.T.astype(BF16), xd_b[:, gsl])
        s_ref[:, gsl] = s_prev * jnp.exp(a_last[:, gsl]) + s_new
        pair_w = 2 * SSD_HEAD_DIM
        for j in range(gw // pair_w):
            psl = slice(g * gw + j * pair_w, g * gw + (j + 1) * pair_w)
            blk = a_cum[:, psl]
            blk_t = blk.T
            swp = pltpu.roll(blk, SSD_HEAD_DIM, axis=1)
            xp = xdt[:, psl]
            yd = None
            for half in range(2):
                col = jnp.where(lane_lo, blk, swp) if half == 0 else jnp.where(lane_lo, swp, blk)
                rowv = blk_t[half * SSD_HEAD_DIM:half * SSD_HEAD_DIM + 1, :]
                dec = jnp.exp(jnp.where(tril, col - rowv, NEG_BIG))
                m = (cbm * dec).astype(BF16)
                keep = lane_lo if half == 0 else jnp.logical_not(lane_lo)
                xh = jnp.where(keep, xp, 0.0).astype(BF16)
                part = _dot(m, xh)
                yd = part if yd is None else yd + part
            y_parts.append(yd + y_off[:, j * pair_w:(j + 1) * pair_w])
    y = jnp.concatenate(y_parts, axis=1) + dsk_ref[...] * xs
    y = y * zs_ref[...].astype(F32)
    for g in range(SSD_GROUPS):
        gsl = slice(g * gw, (g + 1) * gw)
        yg = y[:, gsl]
        ms = jnp.mean(yg * yg, axis=-1, keepdims=True)
        y_ref[:, gsl] = (yg * lax.rsqrt(ms + RMS_EPS) * nw_ref[:, gsl]).astype(BF16)


def _ssd(xbc, dtx, zs, conv_w, conv_b, dtb_x, alog_x, dsk_x, norm_w, batch):
    t, d = zs.shape
    n_xbc = xbc.shape[1]
    n_chunks = t // batch // CHUNK
    row = lambda b, c: (b * n_chunks + c, 0)
    vmem = 24 * 1024 * 1024
    return pl.pallas_call(
        _ssd_kernel,
        grid=(batch, n_chunks),
        in_specs=[pl.BlockSpec((CHUNK, n_xbc), row), pl.BlockSpec((CHUNK, d), row),
                  pl.BlockSpec((CHUNK, d), row),
                  _const_spec((SSD_CONV, n_xbc)), _const_spec((1, n_xbc)),
                  _const_spec((1, d)), _const_spec((1, d)), _const_spec((1, d)),
                  _const_spec((1, d))],
        out_specs=pl.BlockSpec((CHUNK, d), row),
        out_shape=jax.ShapeDtypeStruct((t, d), BF16),
        scratch_shapes=[pltpu.VMEM((SSD_STATE, d), F32),
                        pltpu.VMEM((V7X_SUBLANES, n_xbc), F32)],
        compiler_params=_params(2, vmem),
        name="ssd_scan",
    )(xbc, dtx, zs, conv_w, conv_b, dtb_x, alog_x, dsk_x, norm_w)


def _even_out_kernel(alpha, x_ref, ys_ref, ug_ref, vn_ref, ws_ref, sb_ref, wo_ref,
                     lng_ref, lnb_ref, rwt_ref, rb_ref, xo_ref, xb_ref, e_ref, g_ref):
    tm, d = x_ref.shape
    gd = d // SGU_GROUPS
    rows = []
    for c in range(tm // CHUNK):
        rsl = slice(c * CHUNK, (c + 1) * CHUNK)
        cols = []
        for g in range(SGU_GROUPS):
            csl = slice(g * gd, (g + 1) * gd)
            s = _dot(ws_ref[g], vn_ref[rsl, csl]) + sb_ref[:, csl]
            cols.append((ug_ref[rsl, csl].astype(F32) * s).astype(BF16))
        rows.append(jnp.concatenate(cols, axis=1))
    y_sgu = jnp.concatenate(rows, axis=0)
    m = _dot(ys_ref[...], wo_ref[0:d, :]) + _dot(y_sgu, wo_ref[d:2 * d, :])
    xn = _layer_norm_rows(alpha * x_ref[...] + m, lng_ref[...], lnb_ref[...])
    xo_ref[...] = xn
    xb_ref[...] = xn.astype(BF16)
    _route(xn, rwt_ref, rb_ref, e_ref, g_ref)


def _even_out(alpha, x, y_ssd, ug, vn, w_sgu, sgu_bias_x, w_out, ln_g, ln_b, rwt, rb_x):
    t, d = x.shape
    tm = ROW_TILE
    row = lambda i: (i, 0)
    colb = lambda i: (0, i)
    vmem = (2 * tm * d * 4 + 3 * 2 * tm * d * 2 + SGU_GROUPS * CHUNK * CHUNK * 2
            + CHUNK * d * 4 + 2 * d * d * 2 + 2 * tm * d * 6 + 8 * tm * d * 4)
    return pl.pallas_call(
        functools.partial(_even_out_kernel, alpha),
        grid=(t // tm,),
        in_specs=[pl.BlockSpec((tm, d), row), pl.BlockSpec((tm, d), row),
                  pl.BlockSpec((tm, d), row), pl.BlockSpec((tm, d), row),
                  _const_spec((SGU_GROUPS, CHUNK, CHUNK)), _const_spec((CHUNK, d)),
                  _const_spec((2 * d, d)), _const_spec((1, d)), _const_spec((1, d)),
                  _const_spec((N_EXPERTS, d)), _const_spec((N_EXPERTS, tm))],
        out_specs=[pl.BlockSpec((tm, d), row), pl.BlockSpec((tm, d), row),
                   pl.BlockSpec((TOP_K, tm), colb), pl.BlockSpec((TOP_K, tm), colb)],
        out_shape=[jax.ShapeDtypeStruct((t, d), F32), jax.ShapeDtypeStruct((t, d), BF16),
                   jax.ShapeDtypeStruct((TOP_K, t), jnp.int32),
                   jax.ShapeDtypeStruct((TOP_K, t), F32)],
        compiler_params=_params(1, vmem),
        name="even_out_proj",
    )(x, y_ssd, ug, vn, w_sgu, sgu_bias_x, w_out, ln_g, ln_b, rwt, rb_x)


def _odd_kernel(alpha, x_ref, wi_ref, cw_ref, wo_ref, lng_ref, lnb_ref, rwt_ref, rb_ref,
                xo_ref, xb_ref, e_ref, g_ref, tail_ref):
    tm, d = x_ref.shape

    @pl.when(pl.program_id(1) == 0)
    def _():
        tail_ref[...] = jnp.zeros_like(tail_ref)

    x = x_ref[...]
    xb = x.astype(BF16)
    gate_b = _dot(xb, wi_ref[:, 0:d])
    q = _dot(xb, wi_ref[:, d:2 * d]) * _dot(xb, wi_ref[:, 2 * d:3 * d])
    ext = jnp.concatenate([tail_ref[...], q], axis=0)
    tail_ref[...] = q[tm - V7X_SUBLANES:tm]
    acc = cw_ref[CONV_WIDTH - 1:CONV_WIDTH, :] * q
    for k in range(CONV_WIDTH - 1):
        off = V7X_SUBLANES - (CONV_WIDTH - 1) + k
        acc = acc + cw_ref[k:k + 1, :] * ext[off:off + tm]
    m = _dot((gate_b * acc).astype(BF16), wo_ref[...])
    xn = _layer_norm_rows(alpha * x + m, lng_ref[...], lnb_ref[...])
    xo_ref[...] = xn
    xb_ref[...] = xn.astype(BF16)
    _route(xn, rwt_ref, rb_ref, e_ref, g_ref)


def _odd_mixer(alpha, x, w_in, conv_w, w_out, ln_g, ln_b, rwt, rb_x, batch):
    t, d = x.shape
    tm = ROW_TILE
    tiles = t // batch // tm
    row = lambda b, i: (b * tiles + i, 0)
    colb = lambda b, i: (0, b * tiles + i)
    vmem = (2 * tm * d * 4 + 3 * d * d * 2 + d * d * 2 + 2 * tm * d * 6 + 10 * tm * d * 4)
    return pl.pallas_call(
        functools.partial(_odd_kernel, alpha),
        grid=(batch, tiles),
        in_specs=[pl.BlockSpec((tm, d), row), _const_spec((d, 3 * d)),
                  _const_spec((CONV_WIDTH, d)), _const_spec((d, d)),
                  _const_spec((1, d)), _const_spec((1, d)),
                  _const_spec((N_EXPERTS, d)), _const_spec((N_EXPERTS, tm))],
        out_specs=[pl.BlockSpec((tm, d), row), pl.BlockSpec((tm, d), row),
                   pl.BlockSpec((TOP_K, tm), colb), pl.BlockSpec((TOP_K, tm), colb)],
        out_shape=[jax.ShapeDtypeStruct((t, d), F32), jax.ShapeDtypeStruct((t, d), BF16),
                   jax.ShapeDtypeStruct((TOP_K, t), jnp.int32),
                   jax.ShapeDtypeStruct((TOP_K, t), F32)],
        scratch_shapes=[pltpu.VMEM((V7X_SUBLANES, d), F32)],
        compiler_params=_params(2, vmem),
        name="odd_mixer",
    )(x, w_in, conv_w, w_out, ln_g, ln_b, rwt, rb_x)


def _moe_kernel(be_ref, nu_ref, x_ref, wg_ref, wu_ref, wd_ref, o_ref, wgb, wub, wdb):
    i = pl.program_id(0)
    e = be_ref[i]
    prev = be_ref[jnp.maximum(i - 1, 0)]

    @pl.when(jnp.logical_or(i == 0, e != prev))
    def _():
        wgb[...] = wg_ref[0].astype(BF16)
        wub[...] = wu_ref[0].astype(BF16)
        wdb[...] = wd_ref[0].astype(BF16)

    @pl.when(i < nu_ref[0])
    def _():
        x = x_ref[...]
        h = _silu(_dot(x, wgb[...])) * _dot(x, wub[...])
        o_ref[...] = _dot(h.astype(BF16), wdb[...]).astype(o_ref.dtype)

    @pl.when(i >= nu_ref[0])
    def _():
        o_ref[...] = jnp.zeros_like(o_ref)


def _moe_ffn(block_e, n_used, xs, w_gate, w_up, w_down):
    rows, d = xs.shape
    de = w_gate.shape[2]
    bm = MOE_ROWS
    n_blocks = rows // bm
    wmap = lambda i, be, nu: (be[i], 0, 0)
    vmem = (2 * 3 * d * de * 4 + 3 * d * de * 2 + 2 * bm * d * 2 * 2 + 6 * bm * de * 4)
    grid_spec = pltpu.PrefetchScalarGridSpec(
        num_scalar_prefetch=2,
        grid=(n_blocks,),
        in_specs=[pl.BlockSpec((bm, d), lambda i, be, nu: (i, 0)),
                  pl.BlockSpec((1, d, de), wmap), pl.BlockSpec((1, d, de), wmap),
                  pl.BlockSpec((1, de, d), wmap)],
        out_specs=pl.BlockSpec((bm, d), lambda i, be, nu: (i, 0)),
        scratch_shapes=[pltpu.VMEM((d, de), BF16), pltpu.VMEM((d, de), BF16),
                        pltpu.VMEM((de, d), BF16)])
    return pl.pallas_call(
        _moe_kernel,
        grid_spec=grid_spec,
        out_shape=jax.ShapeDtypeStruct((rows, d), BF16),
        compiler_params=_params(1, vmem),
        name="moe_ffn",
    )(block_e, n_used, xs, w_gate, w_up, w_down)


def _combine_kernel(alpha, x_ref, o1_ref, o2_ref, g1_ref, g2_ref, lng_ref, lnb_ref, xo_ref):
    f = g1_ref[...] * o1_ref[...].astype(F32) + g2_ref[...] * o2_ref[...].astype(F32)
    xo_ref[...] = _layer_norm_rows(alpha * x_ref[...] + f, lng_ref[...], lnb_ref[...])


def _combine(alpha, x, o1, o2, g1, g2, ln_g, ln_b):
    t, d = x.shape
    tm = ROW_TILE
    row = lambda i: (i, 0)
    vmem = 2 * tm * d * (4 + 2 + 2 + 4) + 4 * tm * V7X_LANES * 4 + 6 * tm * d * 4
    return pl.pallas_call(
        functools.partial(_combine_kernel, alpha),
        grid=(t // tm,),
        in_specs=[pl.BlockSpec((tm, d), row), pl.BlockSpec((tm, d), row),
                  pl.BlockSpec((tm, d), row), pl.BlockSpec((tm, 1), row),
                  pl.BlockSpec((tm, 1), row), _const_spec((1, d)), _const_spec((1, d))],
        out_specs=pl.BlockSpec((tm, d), row),
        out_shape=jax.ShapeDtypeStruct((t, d), F32),
        compiler_params=_params(1, vmem),
        name="moe_combine",
    )(x, o1, o2, g1, g2, ln_g, ln_b)


def _dispatch_plan(e2, bm):
    k, t = e2.shape
    flat = e2.reshape(-1)
    ids = jnp.arange(N_EXPERTS, dtype=jnp.int32)
    csum = jnp.cumsum((flat[:, None] == ids[None, :]).astype(jnp.int32), axis=0)
    counts = csum[-1]
    rank = jnp.take_along_axis(csum, flat[:, None], axis=1)[:, 0] - 1
    padded = (counts + bm - 1) // bm * bm
    pad_end = jnp.cumsum(padded)
    dest = (pad_end - padded)[flat] + rank
    n_blocks = (k * t) // bm + N_EXPERTS
    blk = jnp.arange(n_blocks, dtype=jnp.int32)
    n_used = pad_end[-1] // bm
    block_e = jnp.minimum(jnp.searchsorted(pad_end, blk * bm, side="right"), N_EXPERTS - 1)
    block_e = jnp.where(blk < n_used, block_e, block_e[jnp.maximum(n_used - 1, 0)])
    tok = jnp.tile(jnp.arange(t, dtype=jnp.int32), k)
    src = jnp.zeros((n_blocks * bm,), jnp.int32).at[dest].set(tok)
    return dest.reshape(k, t), src, block_e.astype(jnp.int32), n_used.astype(jnp.int32).reshape(1)


def _moe_layer(alpha, x, xb, e2, g2, w_gate, w_up, w_down, ln_g, ln_b):
    t, d = x.shape
    dest, src, block_e, n_used = _dispatch_plan(e2, MOE_ROWS)
    xs = jnp.take(xb, src, axis=0)
    ys = _moe_ffn(block_e, n_used, xs, w_gate, w_up, w_down)
    o1 = jnp.take(ys, dest[0], axis=0)
    o2 = jnp.take(ys, dest[1], axis=0)
    return _combine(alpha, x, o1, o2, g2[0].reshape(t, 1), g2[1].reshape(t, 1), ln_g, ln_b)


def kernel(x, even_w_in, even_conv_w, even_conv_b, dt_bias, a_log, d_skip, ssd_norm_w, sgu_ln_g, sgu_ln_b, sgu_w, sgu_b, even_w_out, odd_w_in, odd_conv_w, odd_w_out, mix_ln_g, mix_ln_b, ffn_ln_g, ffn_ln_b, router_w, router_b, moe_w_gate, moe_w_up, moe_w_down):
    batch, seq, d = x.shape
    depth = mix_ln_g.shape[0]
    alpha = float((2 * depth) ** 0.25)
    t = batch * seq
    n_xbc = even_conv_w.shape[2]
    o_dt = d + n_xbc
    o_u = o_dt + SSD_HEADS
    rep = lambda v: jnp.repeat(v, SSD_HEAD_DIM, axis=-1).reshape(1, d).astype(F32)
    rwt = router_w.T.astype(F32)
    rb_x = jnp.broadcast_to(router_b.astype(F32)[:, None], (N_EXPERTS, ROW_TILE))
    h = x.reshape(t, d)
    for i in range(depth):
        j = i // 2
        ln_g = mix_ln_g[i].reshape(1, d)
        ln_b = mix_ln_b[i].reshape(1, d)
        if i % 2 == 0:
            w = even_w_in[j]
            w_all = jnp.concatenate(
                [w[:, 0:o_dt], jnp.repeat(w[:, o_dt:o_u], SSD_HEAD_DIM, axis=1), w[:, o_u:]],
                axis=1).astype(BF16)
            zs, xbc, dtx, ug, vn = _even_in(h, w_all, sgu_ln_g[j].reshape(1, d),
                                            sgu_ln_b[j].reshape(1, d), n_xbc)
            y_ssd = _ssd(xbc, dtx, zs, even_conv_w[j], even_conv_b[j].reshape(1, n_xbc),
                         rep(dt_bias[j]), rep(a_log[j]), rep(d_skip[j]),
                         ssd_norm_w[j].reshape(1, d), batch)
            w_sgu = jnp.tril(sgu_w[j]).astype(BF16)
            sgu_bias_x = jnp.repeat(sgu_b[j].T, d // SGU_GROUPS, axis=1).astype(F32)
            h, hb, e2, g2 = _even_out(alpha, h, y_ssd, ug, vn, w_sgu, sgu_bias_x,
                                      even_w_out[j].astype(BF16), ln_g, ln_b, rwt, rb_x)
        else:
            h, hb, e2, g2 = _odd_mixer(alpha, h, odd_w_in[j].astype(BF16), odd_conv_w[j],
                                       odd_w_out[j].astype(BF16), ln_g, ln_b, rwt, rb_x, batch)
        h = _moe_layer(alpha, h, hb, e2, g2, moe_w_gate[i], moe_w_up[i], moe_w_down[i],
                       ffn_ln_g[i].reshape(1, d), ffn_ln_b[i].reshape(1, d))
    return h.reshape(batch, seq, d)
```
